```python
import math, functools
import jax, jax.numpy as jnp
from jax import lax
import numpy as np


D_MODEL = 4096
BATCH = 2
SEQ = 4096
DEPTH = 1
DEC_BATCH = 32
DEC_SEQ = 1
PAST_LEN = 8192
PAGE_SIZE = 128

GLA_HEADS = 4
GLA_DV = (D_MODEL // 2) // GLA_HEADS
GLA_DK = GLA_DV // 2
GLA_GATE_RANK = 16
GLA_TAU = 16.0
GLA_CHUNK = 64
DSA_HEAD_DIM = 128
DSA_HEADS = (D_MODEL // 2) // DSA_HEAD_DIM
DSA_KV_HEADS = 4
IDX_HEADS = 32
IDX_DIM = 128
TOPK_MAX = 256
Q_BLOCK = 128
ROPE_THETA = 10000.0
D_FF = ((8 * D_MODEL + 3 * 256 - 1) // (3 * 256)) * 256
EPS = 1e-6
GLA_QK_W = GLA_HEADS * GLA_DK
GLA_V_W = GLA_HEADS * GLA_DV
DSA_Q_W = DSA_HEADS * DSA_HEAD_DIM
DSA_KV_W = DSA_KV_HEADS * DSA_HEAD_DIM
IDX_Q_W = IDX_HEADS * IDX_DIM
D_IN = 2 * GLA_QK_W + 2 * GLA_V_W + GLA_GATE_RANK + DSA_Q_W + 2 * DSA_KV_W + IDX_Q_W + IDX_DIM + IDX_HEADS
MIX_W = GLA_V_W + DSA_Q_W

kernel_name = 'hymba_gla_dsa_decode_step'


def rms_norm(x, g):
    xf = x.astype(jnp.float32)
    y = xf * lax.rsqrt(jnp.mean(xf * xf, axis=-1, keepdims=True) + EPS)
    return (y * g.astype(jnp.float32)).astype(x.dtype)


def rotary(x, pos):
    half = x.shape[-1] // 2
    inv_freq = jnp.power(ROPE_THETA, -jnp.arange(half, dtype=jnp.float32) / half)
    ang = pos.astype(jnp.float32)[:, None] * inv_freq[None, :]
    cos = jnp.cos(ang)[:, None, :]
    sin = jnp.sin(ang)[:, None, :]
    xf = x.astype(jnp.float32)
    x1, x2 = xf[..., :half], xf[..., half:]
    return jnp.concatenate([x1 * cos - x2 * sin, x2 * cos + x1 * sin], axis=-1).astype(x.dtype)


def gla_mixer(q, k, v, g_out, gate_lr, w_alpha, b_alpha, gla_norm_g, s0):
    B, T, _ = q.shape
    f32 = jnp.float32
    qf = q.reshape(B, T, GLA_HEADS, GLA_DK).astype(f32) * GLA_DK ** -0.5
    kf = k.reshape(B, T, GLA_HEADS, GLA_DK).astype(f32)
    vf = v.reshape(B, T, GLA_HEADS, GLA_DV).astype(f32)
    log_a = jax.nn.log_sigmoid((gate_lr @ w_alpha + b_alpha).astype(f32)) / GLA_TAU
    log_a = log_a.reshape(B, T, GLA_HEADS, GLA_DK)
    C = math.gcd(T, GLA_CHUNK)
    N = T // C

    def to_chunks(a):
        return a.reshape(B, N, C, GLA_HEADS, a.shape[-1]).transpose(1, 0, 3, 2, 4)

    qc, kc, vc = to_chunks(qf), to_chunks(kf), to_chunks(vf)
    bc = jnp.cumsum(to_chunks(log_a), axis=3)
    mask = jnp.tril(jnp.ones((C, C), dtype=bool))

    def step(S, inp):
        q_, k_, v_, b_ = inp
        b_last = b_[:, :, -1:, :]
        qe = q_ * jnp.exp(b_)
        ke = k_ * jnp.exp(-b_)
        o = jnp.einsum('bhck,bhkv->bhcv', qe, S)
        A = jnp.where(mask, jnp.einsum('bhck,bhsk->bhcs', qe, ke), 0.0)
        o = o + jnp.einsum('bhcs,bhsv->bhcv', A, v_)
        kd = k_ * jnp.exp(b_last - b_)
        S = S * jnp.exp(b_last[:, :, 0, :])[..., None] + jnp.einsum('bhck,bhcv->bhkv', kd, v_)
        return S, o

    S, o = lax.scan(step, s0.astype(f32), (qc, kc, vc, bc))
    o = o.transpose(1, 0, 3, 2, 4).reshape(B, T, GLA_HEADS, GLA_DV)
    o = rms_norm(o, gla_norm_g) * jax.nn.silu(g_out.reshape(B, T, GLA_HEADS, GLA_DV).astype(f32))
    return o.reshape(B, T, GLA_V_W).astype(q.dtype), S.astype(s0.dtype)


def sparse_attend(q, kg, vg, valid):
    B, T, Hq, D = q.shape
    G = Hq // DSA_KV_HEADS
    qg = q.reshape(B, T, DSA_KV_HEADS, G, D)
    s = jnp.einsum('btgjd,btkgd->btgjk', qg, kg).astype(jnp.float32) * D ** -0.5
    s = jnp.where(valid[:, :, None, None, :], s, -jnp.inf)
    p = jax.nn.softmax(s, axis=-1)
    o = jnp.einsum('btgjk,btkgd->btgjd', p.astype(vg.dtype), vg)
    return o.reshape(B, T, Hq * D)


def indexer_scores(iq, ik_all, iw):
    rel = jax.nn.relu(jnp.einsum('bthd,bsd->btsh', iq, ik_all).astype(jnp.float32))
    return jnp.einsum('btsh,bth->bts', rel, iw.astype(jnp.float32))


def dsa_prompt(q, k, v, iq, ik, iw, pos):
    B, T = q.shape[:2]
    topk = min(TOPK_MAX, T // 4)
    nb = T // Q_BLOCK
    key_pos = jnp.arange(T)

    def blocks(a):
        return a.reshape((B, nb, Q_BLOCK) + a.shape[2:]).swapaxes(0, 1)

    def one_block(inp):
        qb, iqb, iwb, qp = inp
        score = indexer_scores(iqb, ik, iwb)
        score = jnp.where(key_pos[None, None, :] <= qp[None, :, None], score, -jnp.inf)
        _, sel = lax.top_k(score, topk)
        valid = sel <= qp[None, :, None]
        kg = jax.vmap(lambda rows, ids: rows[ids])(k, sel)
        vg = jax.vmap(lambda rows, ids: rows[ids])(v, sel)
        return sparse_attend(qb, kg, vg, valid)

    out = lax.map(one_block, (blocks(q), blocks(iq), blocks(iw), pos.reshape(nb, Q_BLOCK)))
    return out.swapaxes(0, 1).reshape(B, T, -1)


def dsa_sample(q, k_new, v_new, iq, ik_new, iw, pos, cache_k, cache_v, cache_idx_k, page_table):
    B, T = q.shape[:2]
    L = PAST_LEN + T
    ik_past = cache_idx_k[page_table].reshape(B, PAST_LEN, IDX_DIM)
    ik_all = jnp.concatenate([ik_past, ik_new.astype(ik_past.dtype)], axis=1)
    score = indexer_scores(iq, ik_all, iw)
    key_pos = jnp.arange(L)
    score = jnp.where(key_pos[None, None, :] <= pos[None, :, None], score, -jnp.inf)
    topk = min(TOPK_MAX, L // 4)
    _, sel = lax.top_k(score, topk)
    valid = sel <= pos[None, :, None]
    is_past = sel < PAST_LEN
    ps = jnp.minimum(sel, PAST_LEN - 1)
    phys = jnp.take_along_axis(page_table, (ps // PAGE_SIZE).reshape(B, -1), axis=1).reshape(B, T, topk)
    off = ps % PAGE_SIZE
    ns = jnp.clip(sel - PAST_LEN, 0, T - 1)
    kg = jnp.where(is_past[..., None, None], cache_k[phys, off],
                   jax.vmap(lambda rows, ids: rows[ids])(k_new, ns).astype(cache_k.dtype))
    vg = jnp.where(is_past[..., None, None], cache_v[phys, off],
                   jax.vmap(lambda rows, ids: rows[ids])(v_new, ns).astype(cache_v.dtype))
    return sparse_attend(q, kg, vg, valid)


def decoder_layer(x, pos, gla_s0, attend, norm1_g, w_in, w_alpha, b_alpha, gla_norm_g, w_o,
                  norm2_g, w_gate, w_up, w_down):
    B, T, _ = x.shape
    h = rms_norm(x, norm1_g)
    splits = np.cumsum([GLA_QK_W, GLA_QK_W, GLA_V_W, GLA_V_W, GLA_GATE_RANK, DSA_Q_W, DSA_KV_W,
                        DSA_KV_W, IDX_Q_W, IDX_DIM, IDX_HEADS])[:-1].tolist()
    gq, gk, gv, gg, glr, dq, dk, dv, iq, ik, iw = jnp.split(h @ w_in, splits, axis=-1)
    o_gla, s_new = gla_mixer(gq, gk, gv, gg, glr, w_alpha, b_alpha, gla_norm_g, gla_s0)
    dq = rotary(dq.reshape(B, T, DSA_HEADS, DSA_HEAD_DIM), pos)
    dk = rotary(dk.reshape(B, T, DSA_KV_HEADS, DSA_HEAD_DIM), pos)
    dv = dv.reshape(B, T, DSA_KV_HEADS, DSA_HEAD_DIM)
    iq = rotary(iq.reshape(B, T, IDX_HEADS, IDX_DIM), pos)
    ik = rotary(ik.reshape(B, T, 1, IDX_DIM), pos)[:, :, 0, :]
    iw = iw * IDX_HEADS ** -0.5
    o_dsa = attend(dq, dk, dv, iq, ik, iw, pos)
    x = x + jnp.concatenate([o_gla, o_dsa.astype(o_gla.dtype)], axis=-1) @ w_o
    h = rms_norm(x, norm2_g)
    x = x + (jax.nn.silu(h @ w_gate) * (h @ w_up)) @ w_down
    return x, s_new, dk, dv, ik


def setup_inputs(seed: int = 0) -> dict:
    key = jax.random.key(seed)
    ks = jax.random.split(key, 24)
    f32 = jnp.float32
    n_pages = PAST_LEN // PAGE_SIZE
    n_phys = (DEC_BATCH * n_pages * 5) // 4

    def nrm(k, shape, scale):
        return jax.random.normal(k, shape, f32) * scale

    page_table = jax.random.permutation(ks[6], n_phys)[:DEC_BATCH * n_pages]
    page_table = page_table.reshape(DEC_BATCH, n_pages).astype(jnp.int32)
    return {
        'x_prompt': nrm(ks[0], (BATCH, SEQ, D_MODEL), 1.0),
        'x_sample': nrm(ks[1], (DEC_BATCH, DEC_SEQ, D_MODEL), 1.0),
        'cache_k': nrm(ks[2], (DEPTH, n_phys, PAGE_SIZE, DSA_KV_HEADS, DSA_HEAD_DIM), 1.0),
        'cache_v': nrm(ks[3], (DEPTH, n_phys, PAGE_SIZE, DSA_KV_HEADS, DSA_HEAD_DIM), 1.0),
        'cache_idx_k': nrm(ks[4], (DEPTH, n_phys, PAGE_SIZE, IDX_DIM), 1.0),
        'state_gla': nrm(ks[5], (DEPTH, DEC_BATCH, GLA_HEADS, GLA_DK, GLA_DV), 0.5),
        'page_table': page_table,
        'norm1_g': 1.0 + nrm(ks[7], (DEPTH, D_MODEL), 0.02),
        'w_in': nrm(ks[8], (DEPTH, D_MODEL, D_IN), D_MODEL ** -0.5),
        'w_alpha': nrm(ks[9], (DEPTH, GLA_GATE_RANK, GLA_QK_W), GLA_GATE_RANK ** -0.5),
        'b_alpha': nrm(ks[10], (DEPTH, GLA_QK_W), 0.1),
        'gla_norm_g': 1.0 + nrm(ks[11], (DEPTH, GLA_DV), 0.02),
        'w_o': nrm(ks[12], (DEPTH, MIX_W, D_MODEL), MIX_W ** -0.5),
        'norm2_g': 1.0 + nrm(ks[13], (DEPTH, D_MODEL), 0.02),
        'w_gate': nrm(ks[14], (DEPTH, D_MODEL, D_FF), D_MODEL ** -0.5),
        'w_up': nrm(ks[15], (DEPTH, D_MODEL, D_FF), D_MODEL ** -0.5),
        'w_down': nrm(ks[16], (DEPTH, D_FF, D_MODEL), D_FF ** -0.5),
        'final_norm_g': 1.0 + nrm(ks[17], (D_MODEL,), 0.02),
    }


def reference(x_prompt, x_sample, cache_k, cache_v, cache_idx_k, state_gla, page_table,
              norm1_g, w_in, w_alpha, b_alpha, gla_norm_g, w_o, norm2_g, w_gate, w_up, w_down,
              final_norm_g):
    b_p, t_p = x_prompt.shape[:2]
    t_s = x_sample.shape[1]
    pos_p = jnp.arange(t_p, dtype=jnp.int32)
    pos_s = PAST_LEN + jnp.arange(t_s, dtype=jnp.int32)
    hp, hs = x_prompt, x_sample
    gla_p, k_p, v_p, ik_p = [], [], [], []
    gla_s, k_s, v_s, ik_s = [], [], [], []
    for l in range(DEPTH):
        w = (norm1_g[l], w_in[l], w_alpha[l], b_alpha[l], gla_norm_g[l], w_o[l], norm2_g[l],
             w_gate[l], w_up[l], w_down[l])
        s0 = jnp.zeros((b_p, GLA_HEADS, GLA_DK, GLA_DV), x_prompt.dtype)
        hp, s_new, k_new, v_new, ik_new = decoder_layer(hp, pos_p, s0, dsa_prompt, *w)
        gla_p.append(s_new); k_p.append(k_new); v_p.append(v_new); ik_p.append(ik_new)
        attend_s = functools.partial(dsa_sample, cache_k=cache_k[l], cache_v=cache_v[l],
                                     cache_idx_k=cache_idx_k[l], page_table=page_table)
        hs, s_new, k_new, v_new, ik_new = decoder_layer(hs, pos_s, state_gla[l], attend_s, *w)
        gla_s.append(s_new); k_s.append(k_new); v_s.append(v_new); ik_s.append(ik_new)
    y_prompt = rms_norm(hp, final_norm_g)
    y_sample = rms_norm(hs, final_norm_g)
    return (y_prompt, y_sample, jnp.stack(gla_p), jnp.stack(k_p), jnp.stack(v_p), jnp.stack(ik_p),
            jnp.stack(gla_s), jnp.stack(k_s), jnp.stack(v_s), jnp.stack(ik_s))
```

```python
import functools
import math

import jax
import jax.numpy as jnp
from jax import lax
from jax.experimental import pallas as pl
from jax.experimental.pallas import tpu as pltpu

F32 = jnp.float32
BF16 = jnp.bfloat16
I32 = jnp.int32

LANES = 128
SUBLANES = 8
V7X_SCOPED_VMEM_BYTES = 60000 * 1024

GLA_TAU = 16.0
GLA_CHUNK = 64
TOPK_MAX = 256
ROPE_THETA = 10000.0
EPS = 1e-6
HEAD_DIM = LANES

NEG_BIG = -1e30
INT_MIN = -(2 ** 31)

KEY_CHUNK = 512
Q_TILE = 128
PAGES_PER_STEP = 8

_NT = (((1,), (1,)), ((), ()))
_TA = (((0,), (0,)), ((), ()))


def _cparams(n_axes, vmem_bytes):
    limit = min(max(int(vmem_bytes * 3 // 2), 16 * 1024 * 1024), V7X_SCOPED_VMEM_BYTES)
    return pltpu.CompilerParams(dimension_semantics=("arbitrary",) * n_axes, vmem_limit_bytes=limit)


def _tile(n, t, unit=LANES):
    if n <= t:
        return n
    best = max(c for c in range(unit, t + 1, unit) if n % c == 0)
    return best


def _dot(a, b):
    return jnp.dot(a, b, preferred_element_type=F32)


def _rope(y, cos, sin):
    return y * cos + pltpu.roll(y, HEAD_DIM // 2, 1) * sin


def _sortable(x):
    i = lax.bitcast_convert_type(x, I32)
    return i ^ ((i >> 31) & jnp.int32(0x7FFFFFFF))


def _log_sigmoid(z):
    return jnp.minimum(z, 0.0) - jnp.log1p(jnp.exp(-jnp.abs(z)))


def _silu(g):
    return g * jax.nn.sigmoid(g)


def _pad_rows8(r):
    first = lax.broadcasted_iota(I32, (SUBLANES, r.shape[1]), 0) == 0
    return jnp.where(first, jnp.broadcast_to(r, (SUBLANES, r.shape[1])), jnp.zeros((), r.dtype))


def _row_to_col(r):
    return jnp.broadcast_to(r, (SUBLANES, r.shape[1])).T[:, 0:1]


def _rmsnorm_kernel(x_ref, g_ref, o_ref):
    x = x_ref[...]
    y = x * lax.rsqrt(jnp.mean(x * x, axis=-1, keepdims=True) + EPS)
    o_ref[...] = (y * g_ref[...]).astype(o_ref.dtype)


def _rmsnorm(x, g, out_dtype, tm):
    rows, d = x.shape
    tm = _tile(rows, tm, SUBLANES)
    vmem = 2 * tm * d * (4 + jnp.dtype(out_dtype).itemsize)
    return pl.pallas_call(
        _rmsnorm_kernel,
        grid=(rows // tm,),
        in_specs=[pl.BlockSpec((tm, d), lambda i: (i, 0)), pl.BlockSpec((1, d), lambda i: (0, 0))],
        out_specs=pl.BlockSpec((tm, d), lambda i: (i, 0)),
        out_shape=jax.ShapeDtypeStruct((rows, d), out_dtype),
        compiler_params=_cparams(1, vmem),
        name="rmsnorm",
    )(x, g.reshape(1, d).astype(F32))


def _mm_kernel(x_ref, w_ref, o_ref):
    o_ref[...] = _dot(x_ref[...], w_ref[...]).astype(o_ref.dtype)


def _matmul(x, w, out_dtype, tm, tn):
    m, k = x.shape
    n = w.shape[1]
    tm, tn = _tile(m, tm, SUBLANES), _tile(n, tn)
    vmem = 2 * (tm * k * 2 + k * tn * 2 + tm * tn * 4) + tm * tn * 4
    return pl.pallas_call(
        _mm_kernel,
        grid=(m // tm, n // tn),
        in_specs=[pl.BlockSpec((tm, k), lambda i, j: (i, 0)), pl.BlockSpec((k, tn), lambda i, j: (0, j))],
        out_specs=pl.BlockSpec((tm, tn), lambda i, j: (i, j)),
        out_shape=jax.ShapeDtypeStruct((m, n), out_dtype),
        compiler_params=_cparams(2, vmem),
        name="proj",
    )(x, w)


def _mm_rope_kernel(x_ref, w_ref, cos_ref, sin_ref, o_ref):
    acc = _dot(x_ref[...], w_ref[...])
    cos, sin = cos_ref[...], sin_ref[...]
    for j in range(acc.shape[1] // HEAD_DIM):
        sl = slice(j * HEAD_DIM, (j + 1) * HEAD_DIM)
        o_ref[:, sl] = _rope(acc[:, sl], cos, sin).astype(o_ref.dtype)


def _matmul_rope(x, w, cos, sin, tm, tn):
    m, k = x.shape
    n = w.shape[1]
    tm, tn = _tile(m, tm, SUBLANES), _tile(n, tn)
    nper = cos.shape[0] // tm
    vmem = 2 * (tm * k * 2 + k * tn * 2 + tm * tn * 2 + 2 * tm * LANES * 4) + tm * tn * 4
    tab = pl.BlockSpec((tm, LANES), lambda i, j: (i % nper, 0))
    return pl.pallas_call(
        _mm_rope_kernel,
        grid=(m // tm, n // tn),
        in_specs=[pl.BlockSpec((tm, k), lambda i, j: (i, 0)), pl.BlockSpec((k, tn), lambda i, j: (0, j)), tab, tab],
        out_specs=pl.BlockSpec((tm, tn), lambda i, j: (i, j)),
        out_shape=jax.ShapeDtypeStruct((m, n), BF16),
        compiler_params=_cparams(2, vmem),
        name="proj_rope",
    )(x, w, cos, sin)


def _kv_proj_kernel(x_ref, w_ref, cos_ref, sin_ref, *outs, kvw, rank, idx_heads, transposed):
    k_ref, v_ref, ik_ref, kbf_ref, ikbf_ref, sm_ref = outs[:6]
    acc = _dot(x_ref[...], w_ref[...])
    cos, sin = cos_ref[...], sin_ref[...]
    for j in range(kvw // HEAD_DIM):
        sl = slice(j * HEAD_DIM, (j + 1) * HEAD_DIM)
        r = _rope(acc[:, sl], cos, sin)
        k_ref[:, sl] = r
        kbf_ref[:, sl] = r.astype(BF16)
    v = acc[:, kvw:2 * kvw]
    v_ref[...] = v
    r = _rope(acc[:, 2 * kvw:2 * kvw + HEAD_DIM], cos, sin)
    ik_ref[...] = r
    ikbf_ref[...] = r.astype(BF16)
    s = acc[:, 2 * kvw + HEAD_DIM:]
    lane = lax.broadcasted_iota(I32, s.shape, 1)
    s = s * jnp.where((lane >= rank) & (lane < rank + idx_heads), idx_heads ** -0.5, 1.0)
    sm_ref[...] = s
    if transposed:
        vt_ref, smt_ref = outs[6:]
        vt_ref[0] = v.T.astype(BF16)
        smt_ref[...] = s.T


def _kv_proj(x, w, cos, sin, tm, kvw, rank, idx_heads, transposed):
    m, k = x.shape
    n = w.shape[1]
    tm = min(tm, m)
    nper = cos.shape[0] // tm
    row = lambda width: pl.BlockSpec((tm, width), lambda i: (i, 0))
    out_specs = [row(kvw), row(kvw), row(LANES), row(kvw), row(LANES), row(LANES)]
    out_shape = [
        jax.ShapeDtypeStruct((m, kvw), F32), jax.ShapeDtypeStruct((m, kvw), F32),
        jax.ShapeDtypeStruct((m, LANES), F32), jax.ShapeDtypeStruct((m, kvw), BF16),
        jax.ShapeDtypeStruct((m, LANES), BF16), jax.ShapeDtypeStruct((m, LANES), F32),
    ]
    if transposed:
        out_specs += [pl.BlockSpec((1, kvw, tm), lambda i: (i, 0, 0)), pl.BlockSpec((LANES, tm), lambda i: (0, i))]
        out_shape += [jax.ShapeDtypeStruct((m // tm, kvw, tm), BF16), jax.ShapeDtypeStruct((LANES, m), F32)]
    vmem = 2 * (tm * k * 2 + k * n * 2 + tm * n * 8) + tm * n * 8
    tab = pl.BlockSpec((tm, LANES), lambda i: (i % nper, 0))
    return pl.pallas_call(
        functools.partial(_kv_proj_kernel, kvw=kvw, rank=rank, idx_heads=idx_heads, transposed=transposed),
        grid=(m // tm,),
        in_specs=[pl.BlockSpec((tm, k), lambda i: (i, 0)), pl.BlockSpec((k, n), lambda i: (0, 0)), tab, tab],
        out_specs=out_specs,
        out_shape=out_shape,
        compiler_params=_cparams(1, vmem),
        name="kv_proj",
    )(x, w, cos, sin)


def _oproj_kernel(a_ref, b_ref, wa_ref, wb_ref, r_ref, o_ref):
    o_ref[...] = r_ref[...] + (_dot(a_ref[...], wa_ref[...]) + _dot(b_ref[...], wb_ref[...]))


def _oproj(a, b, w, res, tm, tn):
    m, ka = a.shape
    kb = b.shape[1]
    n = w.shape[1]
    assert ka == kb, (ka, kb)
    tm, tn = _tile(m, tm, SUBLANES), _tile(n, tn)
    vmem = 2 * (2 * tm * ka * 2 + 2 * ka * tn * 2 + 2 * tm * tn * 4) + tm * tn * 4
    return pl.pallas_call(
        _oproj_kernel,
        grid=(m // tm, n // tn),
        in_specs=[
            pl.BlockSpec((tm, ka), lambda i, j: (i, 0)), pl.BlockSpec((tm, kb), lambda i, j: (i, 0)),
            pl.BlockSpec((ka, tn), lambda i, j: (0, j)), pl.BlockSpec((kb, tn), lambda i, j: (1, j)),
            pl.BlockSpec((tm, tn), lambda i, j: (i, j)),
        ],
        out_specs=pl.BlockSpec((tm, tn), lambda i, j: (i, j)),
        out_shape=jax.ShapeDtypeStruct((m, n), F32),
        compiler_params=_cparams(2, vmem),
        name="out_proj",
    )(a, b, w, w, res)


def _ffn_up_kernel(x_ref, wg_ref, wu_ref, o_ref):
    x = x_ref[...]
    g = _dot(x, wg_ref[...])
    o_ref[...] = (_silu(g) * _dot(x, wu_ref[...])).astype(o_ref.dtype)


def _ffn_up(x, wg, wu, tm, tn):
    m, k = x.shape
    n = wg.shape[1]
    tm, tn = _tile(m, tm, SUBLANES), _tile(n, tn)
    vmem = 2 * (tm * k * 2 + 2 * k * tn * 2 + tm * tn * 2) + 2 * tm * tn * 4
    return pl.pallas_call(
        _ffn_up_kernel,
        grid=(m // tm, n // tn),
        in_specs=[pl.BlockSpec((tm, k), lambda i, j: (i, 0)), pl.BlockSpec((k, tn), lambda i, j: (0, j)),
                  pl.BlockSpec((k, tn), lambda i, j: (0, j))],
        out_specs=pl.BlockSpec((tm, tn), lambda i, j: (i, j)),
        out_shape=jax.ShapeDtypeStruct((m, n), BF16),
        compiler_params=_cparams(2, vmem),
        name="ffn_up",
    )(x, wg, wu)


def _ffn_down_kernel(a_ref, w_ref, r_ref, o_ref, acc_ref):
    kk = pl.program_id(2)

    @pl.when(kk == 0)
    def _():
        acc_ref[...] = jnp.zeros_like(acc_ref)

    acc_ref[...] += _dot(a_ref[...], w_ref[...])

    @pl.when(kk == pl.num_programs(2) - 1)
    def _():
        o_ref[...] = r_ref[...] + acc_ref[...]


def _ffn_down(a, w, res, tm, tn, tk):
    m, k = a.shape
    n = w.shape[1]
    tm, tn = _tile(m, tm, SUBLANES), _tile(n, tn)
    vmem = 2 * (tm * tk * 2 + tk * tn * 2 + 2 * tm * tn * 4) + 2 * tm * tn * 4
    return pl.pallas_call(
        _ffn_down_kernel,
        grid=(m // tm, n // tn, k // tk),
        in_specs=[pl.BlockSpec((tm, tk), lambda i, j, kk: (i, kk)), pl.BlockSpec((tk, tn), lambda i, j, kk: (kk, j)),
                  pl.BlockSpec((tm, tn), lambda i, j, kk: (i, j))],
        out_specs=pl.BlockSpec((tm, tn), lambda i, j, kk: (i, j)),
        out_shape=jax.ShapeDtypeStruct((m, n), F32),
        scratch_shapes=[pltpu.VMEM((tm, tn), F32)],
        compiler_params=_cparams(3, vmem),
        name="ffn_down",
    )(a, w, res)


def _gla_prompt_kernel(q_ref, k_ref, v_ref, g_ref, sm_ref, wa_ref, ba_ref, gn_ref, o_ref, s_out_ref, s_ref, *, dk):
    tb = pl.program_id(2)

    @pl.when(tb == 0)
    def _():
        s_ref[...] = jnp.zeros_like(s_ref)

    c_len = GLA_CHUNK
    row = lax.broadcasted_iota(I32, (c_len, c_len), 0)
    col = lax.broadcasted_iota(I32, (c_len, c_len), 1)
    tril = row >= col
    tril_f = tril.astype(F32)
    wa, ba, gn = wa_ref[...], ba_ref[...], gn_ref[...]

    def chunk(c, carry):
        rows = pl.ds(pl.multiple_of(c * c_len, c_len), c_len)
        q = q_ref[rows, :] * (dk ** -0.5)
        k = k_ref[rows, :]
        v = v_ref[rows, :].astype(BF16)
        z = _dot(sm_ref[rows, :].astype(BF16), wa) + ba
        log_a = _log_sigmoid(z) / GLA_TAU
        bc = jnp.dot(tril_f, log_a, precision=lax.Precision.HIGHEST, preferred_element_type=F32)
        b_last = bc[c_len - 1:c_len, :]
        qe = (q * jnp.exp(bc)).astype(BF16)
        ke = (k * jnp.exp(-bc)).astype(BF16)
        kd = (k * jnp.exp(b_last - bc)).astype(BF16)
        s = s_ref[...]
        o = _dot(qe, s.astype(BF16))
        a = lax.dot_general(qe, ke, _NT, preferred_element_type=F32)
        o = o + _dot(jnp.where(tril, a, 0.0).astype(BF16), v)
        s_ref[...] = s * jnp.exp(_row_to_col(b_last)) + lax.dot_general(kd, v, _TA, preferred_element_type=F32)
        y = o * lax.rsqrt(jnp.mean(o * o, axis=-1, keepdims=True) + EPS) * gn
        o_ref[rows, :] = (y * _silu(g_ref[rows, :])).astype(o_ref.dtype)
        return carry

    lax.fori_loop(0, q_ref.shape[0] // c_len, chunk, 0)

    @pl.when(tb == pl.num_programs(2) - 1)
    def _():
        s_out_ref[0, 0] = s_ref[...]


def _gla_prompt(gla_in, small, w_alpha, b_alpha, gn, batch, seq, heads, dk, dv, tb):
    tb = min(tb, seq)
    nt = seq // tb
    qk = heads * dk
    assert (2 * qk) % dv == 0 and seq % tb == 0 and tb % GLA_CHUNK == 0
    v_blk, g_blk = (2 * qk) // dv, (2 * qk + heads * dv) // dv
    rows = lambda b, h, t: b * nt + t
    vmem = 2 * (tb * (2 * dk + 2 * dv + LANES) * 4 + tb * dv * 2 + dk * dv * 4) + dk * dv * 4
    return pl.pallas_call(
        functools.partial(_gla_prompt_kernel, dk=dk),
        grid=(batch, heads, nt),
        in_specs=[
            pl.BlockSpec((tb, dk), lambda b, h, t: (rows(b, h, t), h)),
            pl.BlockSpec((tb, dk), lambda b, h, t: (rows(b, h, t), heads + h)),
            pl.BlockSpec((tb, dv), lambda b, h, t: (rows(b, h, t), v_blk + h)),
            pl.BlockSpec((tb, dv), lambda b, h, t: (rows(b, h, t), g_blk + h)),
            pl.BlockSpec((tb, LANES), lambda b, h, t: (rows(b, h, t), 0)),
            pl.BlockSpec((LANES, dk), lambda b, h, t: (0, h)),
            pl.BlockSpec((1, dk), lambda b, h, t: (0, h)),
            pl.BlockSpec((1, dv), lambda b, h, t: (0, 0)),
        ],
        out_specs=[
            pl.BlockSpec((tb, dv), lambda b, h, t: (rows(b, h, t), h)),
            pl.BlockSpec((1, 1, dk, dv), lambda b, h, t: (b, h, 0, 0)),
        ],
        out_shape=[jax.ShapeDtypeStruct((batch * seq, heads * dv), BF16),
                   jax.ShapeDtypeStruct((batch, heads, dk, dv), F32)],
        scratch_shapes=[pltpu.VMEM((dk, dv), F32)],
        compiler_params=_cparams(3, vmem),
        name="gla_prompt",
    )(gla_in, gla_in, gla_in, gla_in, small, w_alpha, b_alpha, gn)


def _gla_decode_kernel(x_ref, sm_ref, s0_ref, wa_ref, ba_ref, gn_ref, o_ref, s_out_ref, *, heads, dk, dv):
    x = x_ref[0]
    qk, vw = heads * dk, heads * dv
    z = _dot(_pad_rows8(sm_ref[0]).astype(BF16), wa_ref[...])[0:1] + ba_ref[...]
    log_a = _log_sigmoid(z) / GLA_TAU
    gn = gn_ref[...]
    for h in range(heads):
        q = x[:, h * dk:(h + 1) * dk] * (dk ** -0.5)
        k = x[:, qk + h * dk:qk + (h + 1) * dk]
        v = x[:, 2 * qk + h * dv:2 * qk + (h + 1) * dv]
        g = x[:, 2 * qk + vw + h * dv:2 * qk + vw + (h + 1) * dv]
        b = log_a[:, h * dk:(h + 1) * dk]
        qe = q * jnp.exp(b)
        ke = k * jnp.exp(-b)
        kd = k * jnp.exp(b - b)
        s0 = s0_ref[0, h]
        o = _dot(_pad_rows8(qe).astype(BF16), s0.astype(BF16))[0:1]
        a = jnp.sum(qe.astype(BF16).astype(F32) * ke.astype(BF16).astype(F32), axis=1, keepdims=True)
        v_r = v.astype(BF16)
        o = o + a.astype(BF16).astype(F32) * v_r.astype(F32)
        kv = lax.dot_general(_pad_rows8(kd).astype(BF16), _pad_rows8(v).astype(BF16), _TA,
                             preferred_element_type=F32)
        s_out_ref[0, h] = s0 * jnp.exp(_row_to_col(b)) + kv
        y = o * lax.rsqrt(jnp.mean(o * o, axis=-1, keepdims=True) + EPS) * gn
        o_ref[0, :, h * dv:(h + 1) * dv] = (y * _silu(g)).astype(o_ref.dtype)


def _gla_decode(gla_in, small, state, w_alpha, b_alpha, gn, heads, dk, dv):
    n, width = gla_in.shape
    qk = heads * dk
    vmem = 2 * (width * 4 + 2 * heads * dk * dv * 4 + LANES * qk * 2) + 4 * dk * dv * 4
    whole = lambda *shape: pl.BlockSpec(shape, lambda i: (0,) * len(shape))
    return pl.pallas_call(
        functools.partial(_gla_decode_kernel, heads=heads, dk=dk, dv=dv),
        grid=(n,),
        in_specs=[
            pl.BlockSpec((1, 1, width), lambda i: (i, 0, 0)),
            pl.BlockSpec((1, 1, LANES), lambda i: (i, 0, 0)),
            pl.BlockSpec((1, heads, dk, dv), lambda i: (i, 0, 0, 0)),
            whole(LANES, qk), whole(1, qk), whole(1, dv),
        ],
        out_specs=[pl.BlockSpec((1, 1, heads * dv), lambda i: (i, 0, 0)),
                   pl.BlockSpec((1, heads, dk, dv), lambda i: (i, 0, 0, 0))],
        out_shape=[jax.ShapeDtypeStruct((n, 1, heads * dv), BF16), jax.ShapeDtypeStruct(state.shape, F32)],
        compiler_params=_cparams(1, vmem),
        name="gla_decode",
    )(gla_in.reshape(n, 1, width), small.reshape(n, 1, LANES), state, w_alpha, b_alpha, gn)


def _dsa_prompt_kernel(q_ref, iq_ref, smt_ref, ik_ref, k_ref, vt_ref, o_ref, key_ref, m_ref, l_ref, acc_ref, *,
                       topk, rank, idx_heads, kv_heads, group):
    tq = q_ref.shape[0]
    kc = vt_ref.shape[2]
    t0 = pl.program_id(1) * tq
    n_chunks = (t0 + tq + kc - 1) // kc
    w_t = smt_ref[rank:rank + idx_heads, :]
    key_pos = lax.broadcasted_iota(I32, (kc, tq), 0)
    q_pos = t0 + lax.broadcasted_iota(I32, (kc, tq), 1)

    def chunk_rows(c):
        return pl.ds(pl.multiple_of(c * kc, kc), kc)

    def score_chunk(c, carry):
        ik = ik_ref[chunk_rows(c), :]
        acc = jnp.zeros((kc, tq), F32)
        for hp in range(idx_heads // 2):
            iq2 = jnp.concatenate([iq_ref[:, (2 * hp) * HEAD_DIM:(2 * hp + 1) * HEAD_DIM],
                                   iq_ref[:, (2 * hp + 1) * HEAD_DIM:(2 * hp + 2) * HEAD_DIM]], axis=0)
            r = lax.dot_general(ik, iq2, _NT, preferred_element_type=F32)
            acc = acc + jnp.maximum(r[:, :tq], 0.0) * w_t[2 * hp:2 * hp + 1, :]
            acc = acc + jnp.maximum(r[:, tq:], 0.0) * w_t[2 * hp + 1:2 * hp + 2, :]
        causal = (c * kc + key_pos) <= q_pos
        key_ref[chunk_rows(c), :] = jnp.where(causal, _sortable(acc), INT_MIN)
        return carry

    lax.fori_loop(0, n_chunks, score_chunk, 0)

    def count_ge(cand):
        def body(c, cnt):
            hit = jnp.where(key_ref[chunk_rows(c), :] >= cand, 1.0, 0.0)
            return cnt + jnp.sum(hit.reshape(kc // SUBLANES, SUBLANES, tq), axis=0)

        cnt = lax.fori_loop(0, n_chunks, body, jnp.zeros((SUBLANES, tq), F32))
        return jnp.sum(cnt, axis=0, keepdims=True)

    def bit_step(i, res):
        cand = res | (jnp.int32(1) << (30 - i))
        return jnp.where(count_ge(cand) >= topk, cand, res)

    res = jnp.where(count_ge(jnp.zeros((1, tq), I32)) >= topk, 0, INT_MIN).astype(I32)
    tau = lax.fori_loop(0, 31, bit_step, res)

    m_ref[...] = jnp.full_like(m_ref, NEG_BIG)
    l_ref[...] = jnp.zeros_like(l_ref)
    acc_ref[...] = jnp.zeros_like(acc_ref)
    scale = HEAD_DIM ** -0.5

    def attend_chunk(c, carry):
        rows = chunk_rows(c)
        sel = (key_ref[rows, :] >= tau) & ((c * kc + key_pos) <= q_pos)
        bias = jnp.where(sel, 0.0, NEG_BIG)
        bias2 = jnp.concatenate([bias, bias], axis=1)
        for g in range(kv_heads):
            k_g = k_ref[rows, g * HEAD_DIM:(g + 1) * HEAD_DIM]
            vt_g = vt_ref[c, g * HEAD_DIM:(g + 1) * HEAD_DIM, :]
            for pr in range(group // 2):
                h0 = g * group + 2 * pr
                p = h0 // 2
                q2 = jnp.concatenate([q_ref[:, h0 * HEAD_DIM:(h0 + 1) * HEAD_DIM],
                                      q_ref[:, (h0 + 1) * HEAD_DIM:(h0 + 2) * HEAD_DIM]], axis=0)
                s = lax.dot_general(k_g, q2, _NT, preferred_element_type=F32) * scale + bias2
                m_old = m_ref[p]
                m_new = jnp.maximum(m_old, jnp.max(s, axis=0, keepdims=True))
                alpha = jnp.exp(m_old - m_new)
                pe = jnp.exp(s - m_new)
                l_ref[p] = alpha * l_ref[p] + jnp.sum(pe, axis=0, keepdims=True)
                acc_ref[p] = acc_ref[p] * alpha + _dot(vt_g, pe.astype(BF16))
                m_ref[p] = m_new
        return carry

    lax.fori_loop(0, n_chunks, attend_chunk, 0)

    for p in range(kv_heads * group // 2):
        o = acc_ref[p] / l_ref[p]
        for j in range(2):
            h = 2 * p + j
            o_ref[:, h * HEAD_DIM:(h + 1) * HEAD_DIM] = o[:, j * tq:(j + 1) * tq].T.astype(o_ref.dtype)


def _dsa_prompt(q, iq, small_t, ik, k, vt, batch, seq, topk, rank, idx_heads, kv_heads, group):
    rows = batch * seq
    tq = min(Q_TILE, seq)
    kc = vt.shape[2]
    kvw = kv_heads * HEAD_DIM
    heads = kv_heads * group
    assert group % 2 == 0 and idx_heads % 2 == 0 and seq % kc == 0 and seq % tq == 0 and kc >= topk
    nq = seq // tq
    n_pairs = heads // 2
    vmem = (2 * (tq * heads * HEAD_DIM * 2 * 2 + tq * idx_heads * HEAD_DIM * 2 + LANES * tq * 4
                 + seq * LANES * 2 + 2 * seq * kvw * 2)
            + seq * tq * 4 + n_pairs * (HEAD_DIM + 2 * SUBLANES) * 2 * tq * 4 + 8 * kc * 2 * tq * 4)
    return pl.pallas_call(
        functools.partial(_dsa_prompt_kernel, topk=topk, rank=rank, idx_heads=idx_heads, kv_heads=kv_heads,
                          group=group),
        grid=(batch, nq),
        in_specs=[
            pl.BlockSpec((tq, heads * HEAD_DIM), lambda b, t: (b * nq + t, 0)),
            pl.BlockSpec((tq, idx_heads * HEAD_DIM), lambda b, t: (b * nq + t, 0)),
            pl.BlockSpec((LANES, tq), lambda b, t: (0, b * nq + t)),
            pl.BlockSpec((seq, HEAD_DIM), lambda b, t: (b, 0)),
            pl.BlockSpec((seq, kvw), lambda b, t: (b, 0)),
            pl.BlockSpec((seq // kc, kvw, kc), lambda b, t: (b, 0, 0)),
        ],
        out_specs=pl.BlockSpec((tq, heads * HEAD_DIM), lambda b, t: (b * nq + t, 0)),
        out_shape=jax.ShapeDtypeStruct((rows, heads * HEAD_DIM), BF16),
        scratch_shapes=[
            pltpu.VMEM((seq, tq), I32),
            pltpu.VMEM((n_pairs, 1, 2 * tq), F32),
            pltpu.VMEM((n_pairs, 1, 2 * tq), F32),
            pltpu.VMEM((n_pairs, HEAD_DIM, 2 * tq), F32),
        ],
        compiler_params=_cparams(2, vmem),
        name="dsa_prompt",
    )(q, iq, small_t, ik, k, vt)


def _indexer_weight_col(sm_row, rank, idx_heads):
    return jnp.broadcast_to(sm_row, (SUBLANES, LANES)).T[rank:rank + idx_heads, 0:1]


def _dec_score_kernel(pt_ref, iq_ref, sm_ref, *rest, rank, idx_heads, npg):
    del pt_ref
    pages, o_ref = rest[:npg], rest[npg]
    ik = jnp.concatenate([p[0] for p in pages], axis=0).astype(BF16)
    r = lax.dot_general(iq_ref[0], ik, _NT, preferred_element_type=F32)
    w_col = _indexer_weight_col(sm_ref[0], rank, idx_heads)
    o_ref[0, 0] = jnp.sum(jnp.maximum(r, 0.0) * w_col, axis=0, keepdims=True)


def _dec_scores(page_table, iq, small, cache_ik, rank, idx_heads, npg):
    n, n_pages = page_table.shape
    page = cache_ik.shape[1]
    steps = n_pages // npg
    width = npg * page
    page_spec = lambda j: pl.BlockSpec((1, page, HEAD_DIM),
                                       lambda b, c, pt: (pt[b * n_pages + c * npg + j], 0, 0))
    grid_spec = pltpu.PrefetchScalarGridSpec(
        num_scalar_prefetch=1,
        grid=(n, steps),
        in_specs=[pl.BlockSpec((1, idx_heads, HEAD_DIM), lambda b, c, pt: (b, 0, 0)),
                  pl.BlockSpec((1, 1, LANES), lambda b, c, pt: (b, 0, 0))] + [page_spec(j) for j in range(npg)],
        out_specs=pl.BlockSpec((1, 1, 1, width), lambda b, c, pt: (b, c, 0, 0)),
    )
    vmem = 2 * (npg * page * HEAD_DIM * 4 + idx_heads * HEAD_DIM * 2 + width * 4) + 4 * idx_heads * width * 4
    out = pl.pallas_call(
        functools.partial(_dec_score_kernel, rank=rank, idx_heads=idx_heads, npg=npg),
        grid_spec=grid_spec,
        out_shape=jax.ShapeDtypeStruct((n, steps, 1, width), F32),
        compiler_params=_cparams(2, vmem),
        name="dec_scores",
    )(page_table.reshape(-1), iq.reshape(n, idx_heads, HEAD_DIM), small.reshape(n, 1, LANES),
      *([cache_ik] * npg))
    return out.reshape(n, steps, width)


def _dec_attend_kernel(pt_ref, sc_ref, iq_ref, ikn_ref, sm_ref, q_ref, kn_ref, vn_ref, *rest, topk, rank,
                       idx_heads, kv_heads, group, npg):
    del pt_ref
    k_pages, v_pages, o_ref = rest[:npg], rest[npg:2 * npg], rest[2 * npg]
    tau_ref, selfb_ref, qbd_ref, m_ref, l_ref, acc_ref = rest[2 * npg + 1:]
    step = pl.program_id(1)
    heads = kv_heads * group
    kvw = kv_heads * HEAD_DIM
    scale = HEAD_DIM ** -0.5
    row_group = lax.broadcasted_iota(I32, (heads, kvw), 0) // group
    col_group = lax.broadcasted_iota(I32, (heads, kvw), 1) // HEAD_DIM

    @pl.when(step == 0)
    def _():
        w_col = _indexer_weight_col(sm_ref[0], rank, idx_heads)
        ik_new = jnp.broadcast_to(ikn_ref[0], (SUBLANES, HEAD_DIM)).astype(BF16)
        r_self = lax.dot_general(iq_ref[0], ik_new, _NT, preferred_element_type=F32)[:, 0:1]
        self_key = _sortable(jnp.sum(jnp.maximum(r_self, 0.0) * w_col, axis=0, keepdims=True))
        keys = _sortable(sc_ref[0])

        def count_ge(cand):
            hit = jnp.where(keys >= cand, 1.0, 0.0)
            total = jnp.sum(jnp.sum(hit, axis=0, keepdims=True), axis=1, keepdims=True)
            return total + jnp.where(self_key >= cand, 1.0, 0.0)

        def bit_step(i, res):
            cand = res | (jnp.int32(1) << (30 - i))
            return jnp.where(count_ge(cand) >= topk, cand, res)

        res = jnp.where(count_ge(jnp.zeros((1, 1), I32)) >= topk, 0, INT_MIN).astype(I32)
        tau = lax.fori_loop(0, 31, bit_step, res)
        tau_ref[...] = jnp.broadcast_to(tau, tau_ref.shape)
        selfb_ref[...] = jnp.broadcast_to(jnp.where(self_key >= tau, 0.0, NEG_BIG), selfb_ref.shape)
        q_tiled = jnp.concatenate([q_ref[0]] * kv_heads, axis=1)
        qbd_ref[...] = jnp.where(row_group == col_group, q_tiled.astype(F32), 0.0).astype(BF16)
        m_ref[...] = jnp.full_like(m_ref, NEG_BIG)
        l_ref[...] = jnp.zeros_like(l_ref)
        acc_ref[...] = jnp.zeros_like(acc_ref)

    tau = tau_ref[:, 0:1]
    bias = jnp.where(_sortable(sc_ref[0, pl.ds(step, 1), :]) >= tau, 0.0, NEG_BIG)
    k_c = jnp.concatenate([p[0] for p in k_pages], axis=0).astype(BF16)
    v_c = jnp.concatenate([p[0] for p in v_pages], axis=0).astype(BF16)
    s = lax.dot_general(qbd_ref[...], k_c, _NT, preferred_element_type=F32) * scale + bias
    m_old = m_ref[...]
    m_new = jnp.maximum(m_old, jnp.max(s, axis=1, keepdims=True))
    alpha = jnp.exp(m_old - m_new)
    pe = jnp.exp(s - m_new)
    l_ref[...] = alpha * l_ref[...] + jnp.sum(pe, axis=1, keepdims=True)
    acc_ref[...] = acc_ref[...] * alpha + _dot(pe.astype(BF16), v_c)
    m_ref[...] = m_new

    @pl.when(step == pl.num_programs(1) - 1)
    def _():
        k_new = kn_ref[0].astype(BF16).astype(F32)
        v_new = vn_ref[0].astype(BF16).astype(F32)
        s_self = jnp.sum(qbd_ref[...].astype(F32) * k_new, axis=1, keepdims=True) * scale + selfb_ref[:, 0:1]
        m_prev = m_ref[...]
        m_fin = jnp.maximum(m_prev, s_self)
        a_fin = jnp.exp(m_prev - m_fin)
        p_self = jnp.exp(s_self - m_fin)
        l_fin = a_fin * l_ref[...] + p_self
        o_full = (acc_ref[...] * a_fin + p_self * v_new) / l_fin
        o = jnp.zeros((heads, HEAD_DIM), F32)
        for g in range(kv_heads):
            o = o + jnp.where(row_group[:, :HEAD_DIM] == g, o_full[:, g * HEAD_DIM:(g + 1) * HEAD_DIM], 0.0)
        o_ref[0] = o.astype(o_ref.dtype)


def _dec_attend(page_table, scores, iq, ik_new, small, q, k_new, v_new, cache_k, cache_v, topk, rank, idx_heads,
                kv_heads, group, npg):
    n, n_pages = page_table.shape
    page = cache_k.shape[1]
    steps = n_pages // npg
    width = npg * page
    heads = kv_heads * group
    kvw = kv_heads * HEAD_DIM
    per_req = lambda *shape: pl.BlockSpec((1,) + shape, lambda b, c, pt: (b,) + (0,) * len(shape))
    page_spec = lambda j: pl.BlockSpec((1, page, kvw), lambda b, c, pt: (pt[b * n_pages + c * npg + j], 0, 0))
    grid_spec = pltpu.PrefetchScalarGridSpec(
        num_scalar_prefetch=1,
        grid=(n, steps),
        in_specs=[per_req(steps, width), per_req(idx_heads, HEAD_DIM), per_req(1, HEAD_DIM), per_req(1, LANES),
                  per_req(heads, HEAD_DIM), per_req(1, kvw), per_req(1, kvw)]
        + [page_spec(j) for j in range(npg)] * 2,
        out_specs=per_req(heads, HEAD_DIM),
        scratch_shapes=[
            pltpu.VMEM((1, LANES), I32), pltpu.VMEM((1, LANES), F32), pltpu.VMEM((heads, kvw), BF16),
            pltpu.VMEM((heads, 1), F32), pltpu.VMEM((heads, 1), F32), pltpu.VMEM((heads, kvw), F32),
        ],
    )
    vmem = 2 * (2 * npg * page * kvw * 4 + steps * width * 4) + 2 * width * kvw * 2 + 4 * heads * width * 4
    out = pl.pallas_call(
        functools.partial(_dec_attend_kernel, topk=topk, rank=rank, idx_heads=idx_heads, kv_heads=kv_heads,
                          group=group, npg=npg),
        grid_spec=grid_spec,
        out_shape=jax.ShapeDtypeStruct((n, heads, HEAD_DIM), BF16),
        compiler_params=_cparams(2, vmem),
        name="dec_attend",
    )(page_table.reshape(-1), scores, iq.reshape(n, idx_heads, HEAD_DIM), ik_new.reshape(n, 1, HEAD_DIM),
      small.reshape(n, 1, LANES), q.reshape(n, heads, HEAD_DIM), k_new.reshape(n, 1, kvw),
      v_new.reshape(n, 1, kvw), *([cache_k] * npg), *([cache_v] * npg))
    return out.reshape(n, heads * HEAD_DIM)


def _rope_tables(pos):
    half = HEAD_DIM // 2
    inv_freq = jnp.power(ROPE_THETA, -jnp.arange(half, dtype=F32) / half)
    ang = pos.astype(F32)[:, None] * inv_freq[None, :]
    cos, sin = jnp.cos(ang), jnp.sin(ang)
    return jnp.concatenate([cos, cos], axis=-1), jnp.concatenate([-sin, sin], axis=-1)


def _round_up(x, m):
    return (x + m - 1) // m * m


def _prep_weights(w_in, w_alpha, b_alpha, gla_norm_g, w_o, w_gate, w_up, w_down, sizes):
    qk, vw, rank, dq, kvw, iqw, idx_heads = sizes
    edges = [0]
    for width in (qk, qk, vw, vw, rank, dq, kvw, kvw, iqw, HEAD_DIM, idx_heads):
        edges.append(edges[-1] + width)
    o_glr, o_dq, o_dk, o_iq, o_ik, o_iw = edges[4], edges[5], edges[6], edges[8], edges[9], edges[10]
    assert edges[-1] == w_in.shape[1]
    d = w_in.shape[0]
    pad = jnp.zeros((d, LANES - rank - idx_heads), F32)
    w_gla = w_in[:, :o_glr].astype(BF16)
    w_dq = w_in[:, o_dq:o_dk].astype(BF16)
    w_iq = w_in[:, o_iq:o_ik].astype(BF16)
    w_kv = jnp.concatenate([w_in[:, o_dk:o_iq], w_in[:, o_ik:o_iw], w_in[:, o_glr:o_dq], w_in[:, o_iw:], pad],
                           axis=1).astype(BF16)
    wa = jnp.concatenate([w_alpha, jnp.zeros((LANES - rank, qk), F32)], axis=0).astype(BF16)
    d_ff = w_gate.shape[1]
    ff_pad = _round_up(d_ff, 512) - d_ff
    wg = jnp.pad(w_gate.astype(BF16), ((0, 0), (0, ff_pad)))
    wu = jnp.pad(w_up.astype(BF16), ((0, 0), (0, ff_pad)))
    wd = jnp.pad(w_down.astype(BF16), ((0, ff_pad), (0, 0)))
    return dict(w_gla=w_gla, w_dq=w_dq, w_iq=w_iq, w_kv=w_kv, wa=wa, ba=b_alpha.reshape(1, qk).astype(F32),
                gn=gla_norm_g.reshape(1, -1).astype(F32), w_o=w_o.astype(BF16), wg=wg, wu=wu, wd=wd)


def _mlp_tail(x, mix_a, mix_b, w, norm2_g, tm):
    x1 = _oproj(mix_a, mix_b, w["w_o"], x, tm=tm, tn=512)
    h2 = _rmsnorm(x1, norm2_g, BF16, tm=256)
    act = _ffn_up(h2, w["wg"], w["wu"], tm=tm, tn=512)
    return _ffn_down(act, w["wd"], x1, tm=tm, tn=1024, tk=w["wd"].shape[0] // 4)


def kernel(x_prompt, x_sample, cache_k, cache_v, cache_idx_k, state_gla, page_table, norm1_g, w_in, w_alpha,
           b_alpha, gla_norm_g, w_o, norm2_g, w_gate, w_up, w_down, final_norm_g):
    batch, seq, d = x_prompt.shape
    n_dec, dec_seq, _ = x_sample.shape
    assert dec_seq == 1
    depth, _, gla_heads, dk, dv = state_gla.shape
    _, _, page, kv_heads, head_dim = cache_k.shape
    assert head_dim == HEAD_DIM and cache_idx_k.shape[-1] == HEAD_DIM
    n_pages = page_table.shape[1]
    past_len = n_pages * page
    rank = w_alpha.shape[1]
    qk, vw, kvw = gla_heads * dk, gla_heads * dv, kv_heads * HEAD_DIM
    dq = w_o.shape[1] - vw
    heads = dq // HEAD_DIM
    group = heads // kv_heads
    idx_heads = (w_in.shape[2] - (2 * qk + 2 * vw + rank + dq + 2 * kvw + HEAD_DIM)) // (HEAD_DIM + 1)
    iqw = idx_heads * HEAD_DIM
    sizes = (qk, vw, rank, dq, kvw, iqw, idx_heads)
    topk_p = min(TOPK_MAX, seq // 4)
    topk_s = min(TOPK_MAX, (past_len + dec_seq) // 4)
    npg = min(PAGES_PER_STEP, n_pages)
    rows_p = batch * seq
    tm_p = min(1024, rows_p)
    kc = min(KEY_CHUNK, seq)

    cos_p, sin_p = _rope_tables(jnp.arange(seq, dtype=I32))
    cos_s, sin_s = _rope_tables(jnp.full((n_dec,), past_len, I32))
    pt = page_table.astype(I32)

    xp = x_prompt.reshape(rows_p, d)
    xs = x_sample.reshape(n_dec, d)
    outs = {name: [] for name in ("gla_p", "k_p", "v_p", "ik_p", "gla_s", "k_s", "v_s", "ik_s")}
    for l in range(depth):
        w = _prep_weights(w_in[l], w_alpha[l], b_alpha[l], gla_norm_g[l], w_o[l], w_gate[l], w_up[l], w_down[l],
                          sizes)
        ck = cache_k[l].reshape(-1, page, kvw)
        cv = cache_v[l].reshape(-1, page, kvw)

        h = _rmsnorm(xp, norm1_g[l], BF16, tm=256)
        gla_in = _matmul(h, w["w_gla"], F32, tm=tm_p, tn=512)
        q = _matmul_rope(h, w["w_dq"], cos_p, sin_p, tm=min(tm_p, seq), tn=512)
        iq = _matmul_rope(h, w["w_iq"], cos_p, sin_p, tm=min(tm_p, seq), tn=512)
        k32, v32, ik32, k16, ik16, small, vt, small_t = _kv_proj(h, w["w_kv"], cos_p, sin_p, kc, kvw, rank,
                                                                 idx_heads, transposed=True)
        o_gla, s_new = _gla_prompt(gla_in, small, w["wa"], w["ba"], w["gn"], batch, seq, gla_heads, dk, dv, tb=512)
        o_dsa = _dsa_prompt(q, iq, small_t, ik16, k16, vt, batch, seq, topk_p, rank, idx_heads, kv_heads, group)
        xp = _mlp_tail(xp, o_gla, o_dsa, w, norm2_g[l], tm_p)
        outs["gla_p"].append(s_new)
        outs["k_p"].append(k32.reshape(batch, seq, kv_heads, HEAD_DIM))
        outs["v_p"].append(v32.reshape(batch, seq, kv_heads, HEAD_DIM))
        outs["ik_p"].append(ik32.reshape(batch, seq, HEAD_DIM))

        h = _rmsnorm(xs, norm1_g[l], BF16, tm=n_dec)
        gla_in = _matmul(h, w["w_gla"], F32, tm=n_dec, tn=1024)
        q = _matmul_rope(h, w["w_dq"], cos_s, sin_s, tm=n_dec, tn=1024)
        iq = _matmul_rope(h, w["w_iq"], cos_s, sin_s, tm=n_dec, tn=1024)
        k32, v32, ik32, _, _, small = _kv_proj(h, w["w_kv"], cos_s, sin_s, n_dec, kvw, rank, idx_heads,
                                               transposed=False)
        o_gla, s_new = _gla_decode(gla_in, small, state_gla[l], w["wa"], w["ba"], w["gn"], gla_heads, dk, dv)
        scores = _dec_scores(pt, iq, small, cache_idx_k[l], rank, idx_heads, npg)
        o_dsa = _dec_attend(pt, scores, iq, ik32, small, q, k32, v32, ck, cv, topk_s, rank, idx_heads, kv_heads,
                            group, npg)
        xs = _mlp_tail(xs, o_gla.reshape(n_dec, vw), o_dsa, w, norm2_g[l], n_dec)
        outs["gla_s"].append(s_new)
        outs["k_s"].append(k32.reshape(n_dec, dec_seq, kv_heads, HEAD_DIM))
        outs["v_s"].append(v32.reshape(n_dec, dec_seq, kv_heads, HEAD_DIM))
        outs["ik_s"].append(ik32.reshape(n_dec, dec_seq, HEAD_DIM))

    y_p = _rmsnorm(xp, final_norm_g, F32, tm=256).reshape(batch, seq, d)
    y_s = _rmsnorm(xs, final_norm_g, F32, tm=n_dec).reshape(n_dec, dec_seq, d)
    stack = lambda name: jnp.stack(outs[name])
    return (y_p, y_s, stack("gla_p"), stack("k_p"), stack("v_p"), stack("ik_p"), stack("gla_s"), stack("k_s"),
            stack("v_s"), stack("ik_s"))
```
